```python
import math
import jax
import jax.numpy as jnp
from jax import lax
import numpy as np

D_MODEL = 1024
BATCH = 16
SEQ = 256
DEPTH = 2
DEC_BATCH = 8
DEC_SEQ = 1024
PAST_LEN = 256

GRID_W = 64
EPS = 1e-6
MLA_HEADS = 16
Q_RANK = 384
KV_RANK = 128
NOPE_DIM = 64
ROPE_DIM = 32
V_DIM = 64
ROPE_THETA = 10000.0
Q_BLOCK = 128
SSD_HEADS = 16
SSD_HEAD_DIM = 64
SSD_INNER = SSD_HEADS * SSD_HEAD_DIM
SSD_GROUPS = 2
SSD_STATE = 128
SSD_CONV_W = 5
SSD_CHUNK = 128
SSD_XBC = SSD_INNER + 2 * SSD_GROUPS * SSD_STATE
CONV_CH = 1024
CONV_W = 31
PEER_HEADS = 8
PEER_DQ = 256
N_KEYS = 128
N_EXPERTS = N_KEYS * N_KEYS
PEER_TOPK = 16
PEER_BLOCK = 128
N_BRANCH = 3

OFF_CQ = 0
OFF_CKV = OFF_CQ + Q_RANK
OFF_KR = OFF_CKV + KV_RANK
OFF_Z = OFF_KR + ROPE_DIM
OFF_XBC = OFF_Z + SSD_INNER
OFF_DT = OFF_XBC + SSD_XBC
OFF_GLU = OFF_DT + 2 * SSD_HEADS
N_IN = OFF_GLU + 2 * CONV_CH

kernel_name = "hybrid_mla_ssd_conv_peer_diffusion_step"


def rmsnorm(x, g):
    xf = x.astype(jnp.float32)
    y = xf * lax.rsqrt(jnp.mean(xf * xf, axis=-1, keepdims=True) + EPS)
    return (y * g.astype(jnp.float32)).astype(x.dtype)


def layernorm(x, g, b):
    xf = x.astype(jnp.float32)
    mu = jnp.mean(xf, axis=-1, keepdims=True)
    xc = xf - mu
    y = xc * lax.rsqrt(jnp.mean(xc * xc, axis=-1, keepdims=True) + EPS)
    return (y * g.astype(jnp.float32) + b.astype(jnp.float32)).astype(x.dtype)


def modulation(cvec, w_mod, b_mod):
    m = (jax.nn.silu(cvec) @ w_mod + b_mod)[:, None, :]
    return jnp.split(m, 6, axis=-1)


def dwconv(x, w, b):
    k = w.shape[0]
    y = lax.conv_general_dilated(x, w[:, None, :].astype(x.dtype), (1,), [(k // 2, k // 2)],
                                 dimension_numbers=('NWC', 'WIO', 'NWC'),
                                 feature_group_count=x.shape[-1])
    return y + b


def axial_rope_tables(L):
    rows = L // GRID_W
    row = jnp.repeat(jnp.arange(rows), GRID_W).astype(jnp.float32)
    col = jnp.tile(jnp.arange(GRID_W), rows).astype(jnp.float32)
    nf = ROPE_DIM // 4
    inv = ROPE_THETA ** (-jnp.arange(nf, dtype=jnp.float32) / nf)
    ang = jnp.concatenate([row[:, None] * inv, col[:, None] * inv], axis=-1)
    return jnp.cos(ang), jnp.sin(ang)


def apply_axial_rope(x, cos, sin):
    shp = x.shape
    nf = ROPE_DIM // 4
    xs = x.astype(jnp.float32).reshape(shp[:-1] + (2, 2, nf))
    x1, x2 = xs[..., 0, :], xs[..., 1, :]
    bshape = (shp[1],) + (1,) * (x.ndim - 3) + (2, nf)
    c = cos.reshape(bshape)
    s = sin.reshape(bshape)
    out = jnp.stack([x1 * c - x2 * s, x2 * c + x1 * s], axis=-2)
    return out.reshape(shp).astype(x.dtype)


def mla_queries(cq, g_q_norm, w_uq):
    q = (rmsnorm(cq, g_q_norm) @ w_uq).reshape(cq.shape[:2] + (MLA_HEADS, NOPE_DIM + ROPE_DIM))
    return q[..., :NOPE_DIM], q[..., NOPE_DIM:]


def mla_kv(ckv_n, w_ukv):
    kv = (ckv_n @ w_ukv).reshape(ckv_n.shape[:2] + (MLA_HEADS, NOPE_DIM + V_DIM))
    return kv[..., :NOPE_DIM], kv[..., NOPE_DIM:]


def mla_attention(q_nope, q_rope, k_nope, k_rope, v):
    b, lq, h, _ = q_nope.shape
    nb = lq // Q_BLOCK
    scale = (NOPE_DIM + ROPE_DIM) ** -0.5

    def block(qs):
        qn, qr = qs
        s = (jnp.einsum('bqhd,bkhd->bhqk', qn, k_nope).astype(jnp.float32)
             + jnp.einsum('bqhr,bkr->bhqk', qr, k_rope).astype(jnp.float32))
        p = jax.nn.softmax(s * scale, axis=-1)
        return jnp.einsum('bhqk,bkhd->bqhd', p.astype(v.dtype), v)

    def to_blocks(t):
        return jnp.moveaxis(t.reshape((b, nb, Q_BLOCK) + t.shape[2:]), 1, 0)

    o = lax.map(block, (to_blocks(q_nope), to_blocks(q_rope)))
    return jnp.moveaxis(o, 0, 1).reshape(b, lq, h * V_DIM)


def ssd_chunked(x, dt, A, Bm, Cm, h0):
    f32 = jnp.float32
    b, L, H, P = x.shape
    G, N = Bm.shape[2], Bm.shape[3]
    hg = H // G
    nc = L // SSD_CHUNK
    dt = dt.astype(f32)
    xdt = (x.astype(f32) * dt[..., None]).reshape(b, nc, SSD_CHUNK, G, hg, P)
    a_cum = jnp.cumsum((dt * A.astype(f32)).reshape(b, nc, SSD_CHUNK, G, hg), axis=2)
    Bc = Bm.astype(f32).reshape(b, nc, SSD_CHUNK, G, N)
    Cc = Cm.astype(f32).reshape(b, nc, SSD_CHUNK, G, N)
    tri = jnp.tril(jnp.ones((SSD_CHUNK, SSD_CHUNK), bool))
    seg = a_cum[:, :, :, None] - a_cum[:, :, None, :]
    decay_ls = jnp.exp(jnp.where(tri[:, :, None, None], seg, -jnp.inf))
    cb = jnp.einsum('bclgn,bcsgn->bclsg', Cc, Bc)
    y_diag = jnp.einsum('bclsg,bclsgh,bcsghp->bclghp', cb, decay_ls, xdt)
    decay_end = jnp.exp(a_cum[:, :, -1:] - a_cum)
    chunk_states = jnp.einsum('bclgn,bclgh,bclghp->bcghpn', Bc, decay_end, xdt)
    chunk_decay = jnp.exp(a_cum[:, :, -1])

    def step(s, inp):
        st, dec = inp
        return dec[..., None, None] * s + st, s

    h_final, h_in = lax.scan(step, h0.astype(f32).reshape(b, G, hg, P, N),
                             (jnp.moveaxis(chunk_states, 1, 0), jnp.moveaxis(chunk_decay, 1, 0)))
    h_in = jnp.moveaxis(h_in, 0, 1)
    y_off = jnp.einsum('bclgn,bcghpn,bclgh->bclghp', Cc, h_in, jnp.exp(a_cum))
    return (y_diag + y_off).reshape(b, L, H, P), h_final.reshape(b, H, P, N)


def ssd_bidirectional(xbc_raw, dt_raw, conv_w, conv_b, dt_bias, a_log, d_skip, h0_f, h0_b):
    xbc = jax.nn.silu(dwconv(xbc_raw, conv_w, conv_b))
    b, L, _ = xbc.shape
    gn = SSD_GROUPS * SSD_STATE
    xs = xbc[..., :SSD_INNER].reshape(b, L, SSD_HEADS, SSD_HEAD_DIM)
    Bm = xbc[..., SSD_INNER:SSD_INNER + gn].reshape(b, L, SSD_GROUPS, SSD_STATE)
    Cm = xbc[..., SSD_INNER + gn:].reshape(b, L, SSD_GROUPS, SSD_STATE)
    dt = jax.nn.softplus(dt_raw.astype(jnp.float32).reshape(b, L, 2, SSD_HEADS) + dt_bias.astype(jnp.float32))
    A = -jnp.exp(a_log.astype(jnp.float32))
    y_f, h_f = ssd_chunked(xs, dt[:, :, 0], A[0], Bm, Cm, h0_f)
    flip = lambda t: jnp.flip(t, axis=1)
    y_b, h_b = ssd_chunked(flip(xs), flip(dt[:, :, 1]), A[1], flip(Bm), flip(Cm), h0_b)
    y = y_f + flip(y_b) + d_skip.astype(jnp.float32)[:, None] * xs.astype(jnp.float32)
    return y.reshape(b, L, SSD_INNER), h_f, h_b


def conformer_conv(glu_in, dw_w, dw_b, ln_g, ln_b, w_pw2):
    a, gt = jnp.split(glu_in, 2, axis=-1)
    u = dwconv(a * jax.nn.sigmoid(gt), dw_w, dw_b)
    u = jax.nn.silu(layernorm(u, ln_g, ln_b))
    return u @ w_pw2


def peer(h, w_q, keys, u, v):
    T, D = h.shape

    def block(hb):
        q = (hb @ w_q).reshape(PEER_BLOCK, PEER_HEADS, 2, PEER_DQ // 2)
        s = jnp.einsum('thid,ikd->thik', q, keys).astype(jnp.float32)
        s1, i1 = lax.top_k(s[:, :, 0], PEER_TOPK)
        s2, i2 = lax.top_k(s[:, :, 1], PEER_TOPK)
        cand = (s1[..., :, None] + s2[..., None, :]).reshape(PEER_BLOCK, PEER_HEADS, PEER_TOPK * PEER_TOPK)
        sc, ci = lax.top_k(cand, PEER_TOPK)
        e = (jnp.take_along_axis(i1, ci // PEER_TOPK, axis=-1) * N_KEYS
             + jnp.take_along_axis(i2, ci % PEER_TOPK, axis=-1))
        g = jax.nn.softmax(sc, axis=-1)
        act = jax.nn.gelu(jnp.einsum('thkd,td->thk', u[e], hb).astype(jnp.float32), approximate=False) * g
        return jnp.einsum('thk,thkd->td', act.astype(hb.dtype), v[e])

    out = lax.map(block, h.reshape(T // PEER_BLOCK, PEER_BLOCK, D))
    return out.reshape(T, D)


def trunk_layer(x, cvec, ctx, w_mod, b_mod, g_norm1, g_norm2, w_in, w_gate, b_gate, g_q_norm, w_uq,
                g_kv_norm, w_ukv, w_attn_out, ssd_conv_w, ssd_conv_b, ssd_dt_bias, ssd_a_log, ssd_d,
                g_ssd_norm, w_ssd_out, cv_dw_w, cv_dw_b, cv_ln_g, cv_ln_b, w_cv_pw2, w_out, w_peer_q,
                peer_keys, peer_u, peer_v):
    b, L, _ = x.shape
    sh1, sc1, gt1, sh2, sc2, gt2 = modulation(cvec, w_mod, b_mod)
    h = rmsnorm(x, g_norm1) * (1 + sc1) + sh1
    zin = h @ w_in
    q_nope, q_rope = mla_queries(zin[..., OFF_CQ:OFF_CKV], g_q_norm, w_uq)
    ckv = rmsnorm(zin[..., OFF_CKV:OFF_KR], g_kv_norm)
    k_rope = zin[..., OFF_KR:OFF_Z]
    k_nope, v = mla_kv(ckv, w_ukv)
    if ctx is None:
        h0 = jnp.zeros((b, SSD_HEADS, SSD_HEAD_DIM, SSD_STATE), jnp.float32)
        h0_f, h0_b = h0, h0
        kn_all, kr_all, v_all = k_nope, k_rope, v
    else:
        ckv_c, kr_c, h0_f, h0_b = ctx
        cos, sin = axial_rope_tables(L)
        q_rope = apply_axial_rope(q_rope, cos, sin)
        kn_c, v_c = mla_kv(ckv_c.astype(x.dtype), w_ukv)
        kn_all = jnp.concatenate([kn_c, k_nope], axis=1)
        kr_all = jnp.concatenate([kr_c.astype(x.dtype), apply_axial_rope(k_rope, cos, sin)], axis=1)
        v_all = jnp.concatenate([v_c, v], axis=1)
    o_attn = mla_attention(q_nope, q_rope, kn_all, kr_all, v_all) @ w_attn_out
    y_ssd, h_f, h_b = ssd_bidirectional(zin[..., OFF_XBC:OFF_DT], zin[..., OFF_DT:OFF_GLU], ssd_conv_w, ssd_conv_b,
                                        ssd_dt_bias, ssd_a_log, ssd_d, h0_f, h0_b)
    yz = y_ssd * jax.nn.silu(zin[..., OFF_Z:OFF_XBC].astype(jnp.float32))
    o_ssd = rmsnorm(yz, g_ssd_norm).astype(x.dtype) @ w_ssd_out
    o_conv = conformer_conv(zin[..., OFF_GLU:N_IN], cv_dw_w, cv_dw_b, cv_ln_g, cv_ln_b, w_cv_pw2)
    gates = jax.nn.sigmoid((h @ w_gate + b_gate).astype(jnp.float32)).astype(x.dtype)
    g_a, g_s, g_c = jnp.split(gates, N_BRANCH, axis=-1)
    x = x + gt1 * ((g_a * o_attn + g_s * o_ssd + g_c * o_conv) @ w_out)
    h2 = rmsnorm(x, g_norm2) * (1 + sc2) + sh2
    x = x + gt2 * peer(h2.reshape(b * L, D_MODEL), w_peer_q, peer_keys, peer_u, peer_v).reshape(b, L, D_MODEL)
    if ctx is None:
        return x, (ckv, k_rope, h_f.astype(x.dtype), h_b.astype(x.dtype))
    return x, None


def setup_inputs(seed: int = 0) -> dict:
    key = jax.random.key(seed)
    ks = iter(jax.random.split(key, 64))

    def nrm(shape, s):
        return jax.random.normal(next(ks), shape, jnp.float32) * s

    def gain(shape):
        return 1.0 + nrm(shape, 0.01)

    dt0 = jnp.exp(jax.random.uniform(next(ks), (DEPTH, 2, SSD_HEADS), jnp.float32,
                                     minval=math.log(1e-3), maxval=math.log(1e-1)))
    a0 = jax.random.uniform(next(ks), (DEPTH, 2, SSD_HEADS), jnp.float32, minval=1.0, maxval=16.0)
    return {
        "x_prompt": nrm((BATCH, SEQ, D_MODEL), 1.0),
        "x_sample": nrm((DEC_BATCH, DEC_SEQ, D_MODEL), 1.0),
        "c": nrm((DEC_BATCH, D_MODEL), 1.0),
        "cache_ckv": nrm((DEC_BATCH, DEPTH, PAST_LEN, KV_RANK), 1.0),
        "cache_krope": nrm((DEC_BATCH, DEPTH, PAST_LEN, ROPE_DIM), 1.0),
        "state_ssd_fwd": nrm((DEC_BATCH, DEPTH, SSD_HEADS, SSD_HEAD_DIM, SSD_STATE), 0.1),
        "state_ssd_bwd": nrm((DEC_BATCH, DEPTH, SSD_HEADS, SSD_HEAD_DIM, SSD_STATE), 0.1),
        "c_ctx": nrm((D_MODEL,), 1.0),
        "w_mod": nrm((DEPTH, D_MODEL, 6 * D_MODEL), 0.5 * D_MODEL ** -0.5),
        "b_mod": nrm((DEPTH, 6 * D_MODEL), 0.01),
        "g_norm1": gain((DEPTH, D_MODEL)),
        "g_norm2": gain((DEPTH, D_MODEL)),
        "w_in": nrm((DEPTH, D_MODEL, N_IN), D_MODEL ** -0.5),
        "w_gate": nrm((DEPTH, D_MODEL, N_BRANCH * D_MODEL), D_MODEL ** -0.5),
        "b_gate": nrm((DEPTH, N_BRANCH * D_MODEL), 0.01),
        "g_q_norm": gain((DEPTH, Q_RANK)),
        "w_uq": nrm((DEPTH, Q_RANK, MLA_HEADS * (NOPE_DIM + ROPE_DIM)), Q_RANK ** -0.5),
        "g_kv_norm": gain((DEPTH, KV_RANK)),
        "w_ukv": nrm((DEPTH, KV_RANK, MLA_HEADS * (NOPE_DIM + V_DIM)), KV_RANK ** -0.5),
        "w_attn_out": nrm((DEPTH, MLA_HEADS * V_DIM, D_MODEL), (MLA_HEADS * V_DIM) ** -0.5),
        "ssd_conv_w": nrm((DEPTH, SSD_CONV_W, SSD_XBC), SSD_CONV_W ** -0.5),
        "ssd_conv_b": nrm((DEPTH, SSD_XBC), 0.01),
        "ssd_dt_bias": dt0 + jnp.log(-jnp.expm1(-dt0)),
        "ssd_a_log": jnp.log(a0),
        "ssd_d": gain((DEPTH, SSD_HEADS)),
        "g_ssd_norm": gain((DEPTH, SSD_INNER)),
        "w_ssd_out": nrm((DEPTH, SSD_INNER, D_MODEL), SSD_INNER ** -0.5),
        "cv_dw_w": nrm((DEPTH, CONV_W, CONV_CH), CONV_W ** -0.5),
        "cv_dw_b": nrm((DEPTH, CONV_CH), 0.01),
        "cv_ln_g": gain((DEPTH, CONV_CH)),
        "cv_ln_b": nrm((DEPTH, CONV_CH), 0.01),
        "w_cv_pw2": nrm((DEPTH, CONV_CH, D_MODEL), CONV_CH ** -0.5),
        "w_out": nrm((DEPTH, D_MODEL, D_MODEL), D_MODEL ** -0.5),
        "w_peer_q": nrm((DEPTH, D_MODEL, PEER_HEADS * PEER_DQ), D_MODEL ** -0.5),
        "peer_keys": nrm((DEPTH, 2, N_KEYS, PEER_DQ // 2), (PEER_DQ // 2) ** -0.5),
        "peer_u": nrm((DEPTH, N_EXPERTS, D_MODEL), D_MODEL ** -0.5),
        "peer_v": nrm((DEPTH, N_EXPERTS, D_MODEL), PEER_HEADS ** -0.5),
        "g_final": gain((D_MODEL,)),
    }


def reference(x_prompt, x_sample, c, cache_ckv, cache_krope, state_ssd_fwd, state_ssd_bwd, c_ctx,
              w_mod, b_mod, g_norm1, g_norm2, w_in, w_gate, b_gate, g_q_norm, w_uq, g_kv_norm, w_ukv,
              w_attn_out, ssd_conv_w, ssd_conv_b, ssd_dt_bias, ssd_a_log, ssd_d, g_ssd_norm, w_ssd_out,
              cv_dw_w, cv_dw_b, cv_ln_g, cv_ln_b, w_cv_pw2, w_out, w_peer_q, peer_keys, peer_u, peer_v,
              g_final):
    y_p = x_prompt
    y_s = x_sample
    ckvs, krs, hfs, hbs = [], [], [], []
    for l in range(DEPTH):
        p = dict(w_mod=w_mod[l], b_mod=b_mod[l], g_norm1=g_norm1[l], g_norm2=g_norm2[l], w_in=w_in[l],
                 w_gate=w_gate[l], b_gate=b_gate[l], g_q_norm=g_q_norm[l], w_uq=w_uq[l],
                 g_kv_norm=g_kv_norm[l], w_ukv=w_ukv[l], w_attn_out=w_attn_out[l],
                 ssd_conv_w=ssd_conv_w[l], ssd_conv_b=ssd_conv_b[l], ssd_dt_bias=ssd_dt_bias[l],
                 ssd_a_log=ssd_a_log[l], ssd_d=ssd_d[l], g_ssd_norm=g_ssd_norm[l], w_ssd_out=w_ssd_out[l],
                 cv_dw_w=cv_dw_w[l], cv_dw_b=cv_dw_b[l], cv_ln_g=cv_ln_g[l], cv_ln_b=cv_ln_b[l],
                 w_cv_pw2=w_cv_pw2[l], w_out=w_out[l], w_peer_q=w_peer_q[l], peer_keys=peer_keys[l],
                 peer_u=peer_u[l], peer_v=peer_v[l])
        y_p, ctx_out = trunk_layer(y_p, c_ctx[None, :], None, **p)
        ckv, kr, hf, hb = ctx_out
        ckvs.append(ckv)
        krs.append(kr)
        hfs.append(hf)
        hbs.append(hb)
        y_s, _ = trunk_layer(y_s, c, (cache_ckv[:, l], cache_krope[:, l], state_ssd_fwd[:, l], state_ssd_bwd[:, l]), **p)
    y_prompt = rmsnorm(y_p, g_final)
    y_sample = rmsnorm(y_s, g_final)
    new_ckv = jnp.stack(ckvs, axis=1)
    new_krope = jnp.stack(krs, axis=1)
    new_ssd_fwd = jnp.stack(hfs, axis=1)
    new_ssd_bwd = jnp.stack(hbs, axis=1)
    return (y_prompt, y_sample, new_ckv, new_krope, new_ssd_fwd, new_ssd_bwd)
```

```python
import functools
import math

import jax
import jax.numpy as jnp
from jax import lax
from jax.experimental import pallas as pl
from jax.experimental.pallas import tpu as pltpu

D_MODEL = 1024
BATCH = 16
SEQ = 256
DEPTH = 2
DEC_BATCH = 8
DEC_SEQ = 1024
PAST_LEN = 256

GRID_W = 64
EPS = 1e-6
MLA_HEADS = 16
Q_RANK = 384
KV_RANK = 128
NOPE_DIM = 64
ROPE_DIM = 32
V_DIM = 64
ROPE_THETA = 10000.0
Q_BLOCK = 128
SSD_HEADS = 16
SSD_HEAD_DIM = 64
SSD_INNER = SSD_HEADS * SSD_HEAD_DIM
SSD_GROUPS = 2
SSD_STATE = 128
SSD_CONV_W = 5
SSD_CHUNK = 128
SSD_XBC = SSD_INNER + 2 * SSD_GROUPS * SSD_STATE
CONV_CH = 1024
CONV_W = 31
PEER_HEADS = 8
PEER_DQ = 256
N_KEYS = 128
N_EXPERTS = N_KEYS * N_KEYS
PEER_TOPK = 16
PEER_BLOCK = 128
N_BRANCH = 3

OFF_CQ = 0
OFF_CKV = OFF_CQ + Q_RANK
OFF_KR = OFF_CKV + KV_RANK
OFF_Z = OFF_KR + ROPE_DIM
OFF_XBC = OFF_Z + SSD_INNER
OFF_DT = OFF_XBC + SSD_XBC
OFF_GLU = OFF_DT + 2 * SSD_HEADS
N_IN = OFF_GLU + 2 * CONV_CH


def rmsnorm(x, g):
    xf = x.astype(jnp.float32)
    y = xf * lax.rsqrt(jnp.mean(xf * xf, axis=-1, keepdims=True) + EPS)
    return (y * g.astype(jnp.float32)).astype(x.dtype)


def layernorm(x, g, b):
    xf = x.astype(jnp.float32)
    mu = jnp.mean(xf, axis=-1, keepdims=True)
    xc = xf - mu
    y = xc * lax.rsqrt(jnp.mean(xc * xc, axis=-1, keepdims=True) + EPS)
    return (y * g.astype(jnp.float32) + b.astype(jnp.float32)).astype(x.dtype)


def modulation(cvec, w_mod, b_mod):
    m = (jax.nn.silu(cvec) @ w_mod + b_mod)[:, None, :]
    return jnp.split(m, 6, axis=-1)


def dwconv(x, w, b):
    k = w.shape[0]
    y = lax.conv_general_dilated(x, w[:, None, :].astype(x.dtype), (1,), [(k // 2, k // 2)],
                                 dimension_numbers=('NWC', 'WIO', 'NWC'),
                                 feature_group_count=x.shape[-1])
    return y + b


def axial_rope_tables(L):
    rows = L // GRID_W
    row = jnp.repeat(jnp.arange(rows), GRID_W).astype(jnp.float32)
    col = jnp.tile(jnp.arange(GRID_W), rows).astype(jnp.float32)
    nf = ROPE_DIM // 4
    inv = ROPE_THETA ** (-jnp.arange(nf, dtype=jnp.float32) / nf)
    ang = jnp.concatenate([row[:, None] * inv, col[:, None] * inv], axis=-1)
    return jnp.cos(ang), jnp.sin(ang)


def apply_axial_rope(x, cos, sin):
    shp = x.shape
    nf = ROPE_DIM // 4
    xs = x.astype(jnp.float32).reshape(shp[:-1] + (2, 2, nf))
    x1, x2 = xs[..., 0, :], xs[..., 1, :]
    bshape = (shp[1],) + (1,) * (x.ndim - 3) + (2, nf)
    c = cos.reshape(bshape)
    s = sin.reshape(bshape)
    out = jnp.stack([x1 * c - x2 * s, x2 * c + x1 * s], axis=-2)
    return out.reshape(shp).astype(x.dtype)


def mla_queries(cq, g_q_norm, w_uq):
    q = (rmsnorm(cq, g_q_norm) @ w_uq).reshape(cq.shape[:2] + (MLA_HEADS, NOPE_DIM + ROPE_DIM))
    return q[..., :NOPE_DIM], q[..., NOPE_DIM:]


def mla_kv(ckv_n, w_ukv):
    kv = (ckv_n @ w_ukv).reshape(ckv_n.shape[:2] + (MLA_HEADS, NOPE_DIM + V_DIM))
    return kv[..., :NOPE_DIM], kv[..., NOPE_DIM:]


def mla_attention(q_nope, q_rope, k_nope, k_rope, v):
    b, lq, h, _ = q_nope.shape
    nb = lq // Q_BLOCK
    scale = (NOPE_DIM + ROPE_DIM) ** -0.5

    def block(qs):
        qn, qr = qs
        s = (jnp.einsum('bqhd,bkhd->bhqk', qn, k_nope).astype(jnp.float32)
             + jnp.einsum('bqhr,bkr->bhqk', qr, k_rope).astype(jnp.float32))
        p = jax.nn.softmax(s * scale, axis=-1)
        return jnp.einsum('bhqk,bkhd->bqhd', p.astype(v.dtype), v)

    def to_blocks(t):
        return jnp.moveaxis(t.reshape((b, nb, Q_BLOCK) + t.shape[2:]), 1, 0)

    o = lax.map(block, (to_blocks(q_nope), to_blocks(q_rope)))
    return jnp.moveaxis(o, 0, 1).reshape(b, lq, h * V_DIM)


def ssd_chunked(x, dt, A, Bm, Cm, h0):
    f32 = jnp.float32
    b, L, H, P = x.shape
    G, N = Bm.shape[2], Bm.shape[3]
    hg = H // G
    nc = L // SSD_CHUNK
    dt = dt.astype(f32)
    xdt = (x.astype(f32) * dt[..., None]).reshape(b, nc, SSD_CHUNK, G, hg, P)
    a_cum = jnp.cumsum((dt * A.astype(f32)).reshape(b, nc, SSD_CHUNK, G, hg), axis=2)
    Bc = Bm.astype(f32).reshape(b, nc, SSD_CHUNK, G, N)
    Cc = Cm.astype(f32).reshape(b, nc, SSD_CHUNK, G, N)
    tri = jnp.tril(jnp.ones((SSD_CHUNK, SSD_CHUNK), bool))
    seg = a_cum[:, :, :, None] - a_cum[:, :, None, :]
    decay_ls = jnp.exp(jnp.where(tri[:, :, None, None], seg, -jnp.inf))
    cb = jnp.einsum('bclgn,bcsgn->bclsg', Cc, Bc)
    y_diag = jnp.einsum('bclsg,bclsgh,bcsghp->bclghp', cb, decay_ls, xdt)
    decay_end = jnp.exp(a_cum[:, :, -1:] - a_cum)
    chunk_states = jnp.einsum('bclgn,bclgh,bclghp->bcghpn', Bc, decay_end, xdt)
    chunk_decay = jnp.exp(a_cum[:, :, -1])

    def step(s, inp):
        st, dec = inp
        return dec[..., None, None] * s + st, s

    h_final, h_in = lax.scan(step, h0.astype(f32).reshape(b, G, hg, P, N),
                             (jnp.moveaxis(chunk_states, 1, 0), jnp.moveaxis(chunk_decay, 1, 0)))
    h_in = jnp.moveaxis(h_in, 0, 1)
    y_off = jnp.einsum('bclgn,bcghpn,bclgh->bclghp', Cc, h_in, jnp.exp(a_cum))
    return (y_diag + y_off).reshape(b, L, H, P), h_final.reshape(b, H, P, N)


def ssd_bidirectional(xbc_raw, dt_raw, conv_w, conv_b, dt_bias, a_log, d_skip, h0_f, h0_b):
    xbc = jax.nn.silu(dwconv(xbc_raw, conv_w, conv_b))
    b, L, _ = xbc.shape
    gn = SSD_GROUPS * SSD_STATE
    xs = xbc[..., :SSD_INNER].reshape(b, L, SSD_HEADS, SSD_HEAD_DIM)
    Bm = xbc[..., SSD_INNER:SSD_INNER + gn].reshape(b, L, SSD_GROUPS, SSD_STATE)
    Cm = xbc[..., SSD_INNER + gn:].reshape(b, L, SSD_GROUPS, SSD_STATE)
    dt = jax.nn.softplus(dt_raw.astype(jnp.float32).reshape(b, L, 2, SSD_HEADS) + dt_bias.astype(jnp.float32))
    A = -jnp.exp(a_log.astype(jnp.float32))
    y_f, h_f = ssd_chunked(xs, dt[:, :, 0], A[0], Bm, Cm, h0_f)
    flip = lambda t: jnp.flip(t, axis=1)
    y_b, h_b = ssd_chunked(flip(xs), flip(dt[:, :, 1]), A[1], flip(Bm), flip(Cm), h0_b)
    y = y_f + flip(y_b) + d_skip.astype(jnp.float32)[:, None] * xs.astype(jnp.float32)
    return y.reshape(b, L, SSD_INNER), h_f, h_b


def conformer_conv(glu_in, dw_w, dw_b, ln_g, ln_b, w_pw2):
    a, gt = jnp.split(glu_in, 2, axis=-1)
    u = dwconv(a * jax.nn.sigmoid(gt), dw_w, dw_b)
    u = jax.nn.silu(layernorm(u, ln_g, ln_b))
    return u @ w_pw2


def peer(h, w_q, keys, u, v):
    T, D = h.shape

    def block(hb):
        q = (hb @ w_q).reshape(PEER_BLOCK, PEER_HEADS, 2, PEER_DQ // 2)
        s = jnp.einsum('thid,ikd->thik', q, keys).astype(jnp.float32)
        s1, i1 = lax.top_k(s[:, :, 0], PEER_TOPK)
        s2, i2 = lax.top_k(s[:, :, 1], PEER_TOPK)
        cand = (s1[..., :, None] + s2[..., None, :]).reshape(PEER_BLOCK, PEER_HEADS, PEER_TOPK * PEER_TOPK)
        sc, ci = lax.top_k(cand, PEER_TOPK)
        e = (jnp.take_along_axis(i1, ci // PEER_TOPK, axis=-1) * N_KEYS
             + jnp.take_along_axis(i2, ci % PEER_TOPK, axis=-1))
        g = jax.nn.softmax(sc, axis=-1)
        act = jax.nn.gelu(jnp.einsum('thkd,td->thk', u[e], hb).astype(jnp.float32), approximate=False) * g
        return jnp.einsum('thk,thkd->td', act.astype(hb.dtype), v[e])

    out = lax.map(block, h.reshape(T // PEER_BLOCK, PEER_BLOCK, D))
    return out.reshape(T, D)


def trunk_layer(x, cvec, ctx, w_mod, b_mod, g_norm1, g_norm2, w_in, w_gate, b_gate, g_q_norm, w_uq,
                g_kv_norm, w_ukv, w_attn_out, ssd_conv_w, ssd_conv_b, ssd_dt_bias, ssd_a_log, ssd_d,
                g_ssd_norm, w_ssd_out, cv_dw_w, cv_dw_b, cv_ln_g, cv_ln_b, w_cv_pw2, w_out, w_peer_q,
                peer_keys, peer_u, peer_v):
    b, L, _ = x.shape
    sh1, sc1, gt1, sh2, sc2, gt2 = modulation(cvec, w_mod, b_mod)
    h = rmsnorm(x, g_norm1) * (1 + sc1) + sh1
    zin = h @ w_in
    q_nope, q_rope = mla_queries(zin[..., OFF_CQ:OFF_CKV], g_q_norm, w_uq)
    ckv = rmsnorm(zin[..., OFF_CKV:OFF_KR], g_kv_norm)
    k_rope = zin[..., OFF_KR:OFF_Z]
    k_nope, v = mla_kv(ckv, w_ukv)
    if ctx is None:
        h0 = jnp.zeros((b, SSD_HEADS, SSD_HEAD_DIM, SSD_STATE), jnp.float32)
        h0_f, h0_b = h0, h0
        kn_all, kr_all, v_all = k_nope, k_rope, v
    else:
        ckv_c, kr_c, h0_f, h0_b = ctx
        cos, sin = axial_rope_tables(L)
        q_rope = apply_axial_rope(q_rope, cos, sin)
        kn_c, v_c = mla_kv(ckv_c.astype(x.dtype), w_ukv)
        kn_all = jnp.concatenate([kn_c, k_nope], axis=1)
        kr_all = jnp.concatenate([kr_c.astype(x.dtype), apply_axial_rope(k_rope, cos, sin)], axis=1)
        v_all = jnp.concatenate([v_c, v], axis=1)
    o_attn = mla_attention(q_nope, q_rope, kn_all, kr_all, v_all) @ w_attn_out
    y_ssd, h_f, h_b = ssd_bidirectional(zin[..., OFF_XBC:OFF_DT], zin[..., OFF_DT:OFF_GLU], ssd_conv_w, ssd_conv_b,
                                        ssd_dt_bias, ssd_a_log, ssd_d, h0_f, h0_b)
    yz = y_ssd * jax.nn.silu(zin[..., OFF_Z:OFF_XBC].astype(jnp.float32))
    o_ssd = rmsnorm(yz, g_ssd_norm).astype(x.dtype) @ w_ssd_out
    o_conv = conformer_conv(zin[..., OFF_GLU:N_IN], cv_dw_w, cv_dw_b, cv_ln_g, cv_ln_b, w_cv_pw2)
    gates = jax.nn.sigmoid((h @ w_gate + b_gate).astype(jnp.float32)).astype(x.dtype)
    g_a, g_s, g_c = jnp.split(gates, N_BRANCH, axis=-1)
    x = x + gt1 * ((g_a * o_attn + g_s * o_ssd + g_c * o_conv) @ w_out)
    h2 = rmsnorm(x, g_norm2) * (1 + sc2) + sh2
    x = x + gt2 * peer(h2.reshape(b * L, D_MODEL), w_peer_q, peer_keys, peer_u, peer_v).reshape(b, L, D_MODEL)
    if ctx is None:
        return x, (ckv, k_rope, h_f.astype(x.dtype), h_b.astype(x.dtype))
    return x, None


def _rms_kernel(x_ref, g_ref, o_ref):
    x = x_ref[...]
    o_ref[...] = x * lax.rsqrt(jnp.mean(x * x, axis=-1, keepdims=True) + EPS) * g_ref[...]


def final_rmsnorm(x, g):
    shp = x.shape
    x2 = x.reshape(-1, shp[-1])
    t = x2.shape[0]
    tb = 512
    out = pl.pallas_call(
        _rms_kernel,
        grid=(t // tb,),
        in_specs=[pl.BlockSpec((tb, shp[-1]), lambda i: (i, 0)), pl.BlockSpec((1, shp[-1]), lambda i: (0, 0))],
        out_specs=pl.BlockSpec((tb, shp[-1]), lambda i: (i, 0)),
        out_shape=jax.ShapeDtypeStruct(x2.shape, x2.dtype),
        name="final_rmsnorm",
    )(x2, g.reshape(1, -1))
    return out.reshape(shp)


def kernel(x_prompt, x_sample, c, cache_ckv, cache_krope, state_ssd_fwd, state_ssd_bwd, c_ctx,
           w_mod, b_mod, g_norm1, g_norm2, w_in, w_gate, b_gate, g_q_norm, w_uq, g_kv_norm, w_ukv,
           w_attn_out, ssd_conv_w, ssd_conv_b, ssd_dt_bias, ssd_a_log, ssd_d, g_ssd_norm, w_ssd_out,
           cv_dw_w, cv_dw_b, cv_ln_g, cv_ln_b, w_cv_pw2, w_out, w_peer_q, peer_keys, peer_u, peer_v,
           g_final):
    y_p = x_prompt
    y_s = x_sample
    ckvs, krs, hfs, hbs = [], [], [], []
    for l in range(DEPTH):
        p = dict(w_mod=w_mod[l], b_mod=b_mod[l], g_norm1=g_norm1[l], g_norm2=g_norm2[l], w_in=w_in[l],
                 w_gate=w_gate[l], b_gate=b_gate[l], g_q_norm=g_q_norm[l], w_uq=w_uq[l],
                 g_kv_norm=g_kv_norm[l], w_ukv=w_ukv[l], w_attn_out=w_attn_out[l],
                 ssd_conv_w=ssd_conv_w[l], ssd_conv_b=ssd_conv_b[l], ssd_dt_bias=ssd_dt_bias[l],
                 ssd_a_log=ssd_a_log[l], ssd_d=ssd_d[l], g_ssd_norm=g_ssd_norm[l], w_ssd_out=w_ssd_out[l],
                 cv_dw_w=cv_dw_w[l], cv_dw_b=cv_dw_b[l], cv_ln_g=cv_ln_g[l], cv_ln_b=cv_ln_b[l],
                 w_cv_pw2=w_cv_pw2[l], w_out=w_out[l], w_peer_q=w_peer_q[l], peer_keys=peer_keys[l],
                 peer_u=peer_u[l], peer_v=peer_v[l])
        y_p, ctx_out = trunk_layer(y_p, c_ctx[None, :], None, **p)
        ckv, kr, hf, hb = ctx_out
        ckvs.append(ckv)
        krs.append(kr)
        hfs.append(hf)
        hbs.append(hb)
        y_s, _ = trunk_layer(y_s, c, (cache_ckv[:, l], cache_krope[:, l], state_ssd_fwd[:, l], state_ssd_bwd[:, l]),
                             **p)
    y_prompt = final_rmsnorm(y_p, g_final)
    y_sample = final_rmsnorm(y_s, g_final)
    new_ckv = jnp.stack(ckvs, axis=1)
    new_krope = jnp.stack(krs, axis=1)
    new_ssd_fwd = jnp.stack(hfs, axis=1)
    new_ssd_bwd = jnp.stack(hbs, axis=1)
    return (y_prompt, y_sample, new_ckv, new_krope, new_ssd_fwd, new_ssd_bwd)
```

```python
import functools
import math

import jax
import jax.numpy as jnp
from jax import lax
from jax.experimental import pallas as pl
from jax.experimental.pallas import tpu as pltpu

D_MODEL = 1024
BATCH = 16
SEQ = 256
DEPTH = 2
DEC_BATCH = 8
DEC_SEQ = 1024
PAST_LEN = 256

GRID_W = 64
EPS = 1e-6
MLA_HEADS = 16
Q_RANK = 384
KV_RANK = 128
NOPE_DIM = 64
ROPE_DIM = 32
V_DIM = 64
ROPE_THETA = 10000.0
Q_BLOCK = 128
SSD_HEADS = 16
SSD_HEAD_DIM = 64
SSD_INNER = SSD_HEADS * SSD_HEAD_DIM
SSD_GROUPS = 2
SSD_STATE = 128
SSD_CONV_W = 5
SSD_CHUNK = 128
SSD_XBC = SSD_INNER + 2 * SSD_GROUPS * SSD_STATE
CONV_CH = 1024
CONV_W = 31
PEER_HEADS = 8
PEER_DQ = 256
N_KEYS = 128
N_EXPERTS = N_KEYS * N_KEYS
PEER_TOPK = 16
PEER_BLOCK = 128
N_BRANCH = 3

OFF_CQ = 0
OFF_CKV = OFF_CQ + Q_RANK
OFF_KR = OFF_CKV + KV_RANK
OFF_Z = OFF_KR + ROPE_DIM
OFF_XBC = OFF_Z + SSD_INNER
OFF_DT = OFF_XBC + SSD_XBC
OFF_GLU = OFF_DT + 2 * SSD_HEADS
N_IN = OFF_GLU + 2 * CONV_CH


def rmsnorm(x, g):
    xf = x.astype(jnp.float32)
    y = xf * lax.rsqrt(jnp.mean(xf * xf, axis=-1, keepdims=True) + EPS)
    return (y * g.astype(jnp.float32)).astype(x.dtype)


def layernorm(x, g, b):
    xf = x.astype(jnp.float32)
    mu = jnp.mean(xf, axis=-1, keepdims=True)
    xc = xf - mu
    y = xc * lax.rsqrt(jnp.mean(xc * xc, axis=-1, keepdims=True) + EPS)
    return (y * g.astype(jnp.float32) + b.astype(jnp.float32)).astype(x.dtype)


def modulation(cvec, w_mod, b_mod):
    m = (jax.nn.silu(cvec) @ w_mod + b_mod)[:, None, :]
    return jnp.split(m, 6, axis=-1)


def dwconv(x, w, b):
    k = w.shape[0]
    y = lax.conv_general_dilated(x, w[:, None, :].astype(x.dtype), (1,), [(k // 2, k // 2)],
                                 dimension_numbers=('NWC', 'WIO', 'NWC'),
                                 feature_group_count=x.shape[-1])
    return y + b


def axial_rope_tables(L):
    rows = L // GRID_W
    row = jnp.repeat(jnp.arange(rows), GRID_W).astype(jnp.float32)
    col = jnp.tile(jnp.arange(GRID_W), rows).astype(jnp.float32)
    nf = ROPE_DIM // 4
    inv = ROPE_THETA ** (-jnp.arange(nf, dtype=jnp.float32) / nf)
    ang = jnp.concatenate([row[:, None] * inv, col[:, None] * inv], axis=-1)
    return jnp.cos(ang), jnp.sin(ang)


def apply_axial_rope(x, cos, sin):
    shp = x.shape
    nf = ROPE_DIM // 4
    xs = x.astype(jnp.float32).reshape(shp[:-1] + (2, 2, nf))
    x1, x2 = xs[..., 0, :], xs[..., 1, :]
    bshape = (shp[1],) + (1,) * (x.ndim - 3) + (2, nf)
    c = cos.reshape(bshape)
    s = sin.reshape(bshape)
    out = jnp.stack([x1 * c - x2 * s, x2 * c + x1 * s], axis=-2)
    return out.reshape(shp).astype(x.dtype)


def mla_queries(cq, g_q_norm, w_uq):
    q = (rmsnorm(cq, g_q_norm) @ w_uq).reshape(cq.shape[:2] + (MLA_HEADS, NOPE_DIM + ROPE_DIM))
    return q[..., :NOPE_DIM], q[..., NOPE_DIM:]


def mla_kv(ckv_n, w_ukv):
    kv = (ckv_n @ w_ukv).reshape(ckv_n.shape[:2] + (MLA_HEADS, NOPE_DIM + V_DIM))
    return kv[..., :NOPE_DIM], kv[..., NOPE_DIM:]


def mla_attention(q_nope, q_rope, k_nope, k_rope, v):
    b, lq, h, _ = q_nope.shape
    nb = lq // Q_BLOCK
    scale = (NOPE_DIM + ROPE_DIM) ** -0.5

    def block(qs):
        qn, qr = qs
        s = (jnp.einsum('bqhd,bkhd->bhqk', qn, k_nope).astype(jnp.float32)
             + jnp.einsum('bqhr,bkr->bhqk', qr, k_rope).astype(jnp.float32))
        p = jax.nn.softmax(s * scale, axis=-1)
        return jnp.einsum('bhqk,bkhd->bqhd', p.astype(v.dtype), v)

    def to_blocks(t):
        return jnp.moveaxis(t.reshape((b, nb, Q_BLOCK) + t.shape[2:]), 1, 0)

    o = lax.map(block, (to_blocks(q_nope), to_blocks(q_rope)))
    return jnp.moveaxis(o, 0, 1).reshape(b, lq, h * V_DIM)


def ssd_chunked(x, dt, A, Bm, Cm, h0):
    f32 = jnp.float32
    b, L, H, P = x.shape
    G, N = Bm.shape[2], Bm.shape[3]
    hg = H // G
    nc = L // SSD_CHUNK
    dt = dt.astype(f32)
    xdt = (x.astype(f32) * dt[..., None]).reshape(b, nc, SSD_CHUNK, G, hg, P)
    a_cum = jnp.cumsum((dt * A.astype(f32)).reshape(b, nc, SSD_CHUNK, G, hg), axis=2)
    Bc = Bm.astype(f32).reshape(b, nc, SSD_CHUNK, G, N)
    Cc = Cm.astype(f32).reshape(b, nc, SSD_CHUNK, G, N)
    tri = jnp.tril(jnp.ones((SSD_CHUNK, SSD_CHUNK), bool))
    seg = a_cum[:, :, :, None] - a_cum[:, :, None, :]
    decay_ls = jnp.exp(jnp.where(tri[:, :, None, None], seg, -jnp.inf))
    cb = jnp.einsum('bclgn,bcsgn->bclsg', Cc, Bc)
    y_diag = jnp.einsum('bclsg,bclsgh,bcsghp->bclghp', cb, decay_ls, xdt)
    decay_end = jnp.exp(a_cum[:, :, -1:] - a_cum)
    chunk_states = jnp.einsum('bclgn,bclgh,bclghp->bcghpn', Bc, decay_end, xdt)
    chunk_decay = jnp.exp(a_cum[:, :, -1])

    def step(s, inp):
        st, dec = inp
        return dec[..., None, None] * s + st, s

    h_final, h_in = lax.scan(step, h0.astype(f32).reshape(b, G, hg, P, N),
                             (jnp.moveaxis(chunk_states, 1, 0), jnp.moveaxis(chunk_decay, 1, 0)))
    h_in = jnp.moveaxis(h_in, 0, 1)
    y_off = jnp.einsum('bclgn,bcghpn,bclgh->bclghp', Cc, h_in, jnp.exp(a_cum))
    return (y_diag + y_off).reshape(b, L, H, P), h_final.reshape(b, H, P, N)


def ssd_bidirectional(xbc_raw, dt_raw, conv_w, conv_b, dt_bias, a_log, d_skip, h0_f, h0_b):
    xbc = jax.nn.silu(dwconv(xbc_raw, conv_w, conv_b))
    b, L, _ = xbc.shape
    gn = SSD_GROUPS * SSD_STATE
    xs = xbc[..., :SSD_INNER].reshape(b, L, SSD_HEADS, SSD_HEAD_DIM)
    Bm = xbc[..., SSD_INNER:SSD_INNER + gn].reshape(b, L, SSD_GROUPS, SSD_STATE)
    Cm = xbc[..., SSD_INNER + gn:].reshape(b, L, SSD_GROUPS, SSD_STATE)
    dt = jax.nn.softplus(dt_raw.astype(jnp.float32).reshape(b, L, 2, SSD_HEADS) + dt_bias.astype(jnp.float32))
    A = -jnp.exp(a_log.astype(jnp.float32))
    y_f, h_f = ssd_chunked(xs, dt[:, :, 0], A[0], Bm, Cm, h0_f)
    flip = lambda t: jnp.flip(t, axis=1)
    y_b, h_b = ssd_chunked(flip(xs), flip(dt[:, :, 1]), A[1], flip(Bm), flip(Cm), h0_b)
    y = y_f + flip(y_b) + d_skip.astype(jnp.float32)[:, None] * xs.astype(jnp.float32)
    return y.reshape(b, L, SSD_INNER), h_f, h_b


def conformer_conv(glu_in, dw_w, dw_b, ln_g, ln_b, w_pw2):
    a, gt = jnp.split(glu_in, 2, axis=-1)
    u = dwconv(a * jax.nn.sigmoid(gt), dw_w, dw_b)
    u = jax.nn.silu(layernorm(u, ln_g, ln_b))
    return u @ w_pw2


def peer(h, w_q, keys, u, v):
    T, D = h.shape

    def block(hb):
        q = (hb @ w_q).reshape(PEER_BLOCK, PEER_HEADS, 2, PEER_DQ // 2)
        s = jnp.einsum('thid,ikd->thik', q, keys).astype(jnp.float32)
        s1, i1 = lax.top_k(s[:, :, 0], PEER_TOPK)
        s2, i2 = lax.top_k(s[:, :, 1], PEER_TOPK)
        cand = (s1[..., :, None] + s2[..., None, :]).reshape(PEER_BLOCK, PEER_HEADS, PEER_TOPK * PEER_TOPK)
        sc, ci = lax.top_k(cand, PEER_TOPK)
        e = (jnp.take_along_axis(i1, ci // PEER_TOPK, axis=-1) * N_KEYS
             + jnp.take_along_axis(i2, ci % PEER_TOPK, axis=-1))
        g = jax.nn.softmax(sc, axis=-1)
        act = jax.nn.gelu(jnp.einsum('thkd,td->thk', u[e], hb).astype(jnp.float32), approximate=False) * g
        return jnp.einsum('thk,thkd->td', act.astype(hb.dtype), v[e])

    out = lax.map(block, h.reshape(T // PEER_BLOCK, PEER_BLOCK, D))
    return out.reshape(T, D)


PEER_TB = 512
PEER_EB = 1024
PEER_SUB = 256
LANE_TILE = 128
NEG_INF = float("-inf")
PEER_PAIRS = tuple((r, c) for r in range(PEER_TOPK) for c in range(PEER_TOPK)
                   if (r + 1) * (c + 1) <= PEER_TOPK)
PEER_VMEM_LIMIT = 52 * 1024 * 1024


def _peer_topk_rows(st_ref, hv_ref, hi_ref, vals_ref, ts):
    lanes = pl.ds(ts * LANE_TILE, LANE_TILE)
    kio = lax.broadcasted_iota(jnp.int32, (N_KEYS, LANE_TILE), 0).astype(jnp.float32)
    for i in range(2):
        def head_body(h, carry, i=i):
            row0 = pl.multiple_of((i * PEER_HEADS + h) * N_KEYS, N_KEYS)
            cur = st_ref[pl.ds(row0, N_KEYS), lanes]
            for r in range(PEER_TOPK):
                m = jnp.max(cur, axis=0, keepdims=True)
                idx = jnp.min(jnp.where(cur == m, kio, float(N_KEYS)), axis=0, keepdims=True)
                cur = jnp.where(kio == idx, NEG_INF, cur)
                hv_ref[i, h, pl.ds(r, 1), lanes] = m
                hi_ref[i, h, pl.ds(r, 1), lanes] = idx
            return carry
        lax.fori_loop(0, PEER_HEADS, head_body, 0)
        for h in range(PEER_HEADS):
            for r in range(PEER_TOPK):
                vals_ref[i, r, pl.ds(h, 1), lanes] = hv_ref[i, h, pl.ds(r, 1), lanes]


def _peer_pair_counts(vals_ref, nr_ref, invz_ref, ts):
    lanes = pl.ds(ts * LANE_TILE, LANE_TILE)
    s1 = [vals_ref[0, r, :, lanes] for r in range(PEER_TOPK)]
    s2 = [vals_ref[1, c, :, lanes] for c in range(PEER_TOPK)]
    cand = [s1[r] + s2[c] for (r, c) in PEER_PAIRS]
    counts = [jnp.zeros_like(s1[0]) for _ in range(PEER_TOPK)]
    for _ in range(PEER_TOPK):
        m = functools.reduce(jnp.maximum, cand)
        found = jnp.zeros(m.shape, jnp.bool_)
        for p, (r, c) in enumerate(PEER_PAIRS):
            hit = jnp.logical_and(cand[p] == m, jnp.logical_not(found))
            found = jnp.logical_or(found, hit)
            cand[p] = jnp.where(hit, NEG_INF, cand[p])
            counts[r] = counts[r] + jnp.where(hit, 1.0, 0.0)
    top = s1[0] + s2[0]
    z = jnp.zeros_like(top)
    for (r, c) in PEER_PAIRS:
        z = z + jnp.where(counts[r] > float(c), jnp.exp((s1[r] + s2[c]) - top), 0.0)
    invz = 1.0 / z
    for h in range(PEER_HEADS):
        for r in range(PEER_TOPK):
            nr_ref[h, pl.ds(r, 1), lanes] = counts[r][h:h + 1]
        invz_ref[h, pl.ds(0, 1), lanes] = invz[h:h + 1]


def _peer_planes(st_ref, hv_ref, hi_ref, nr_ref, invz_ref, n_ref, e1n_ref, rank2_ref, e2_ref, ts):
    lanes = pl.ds(ts * LANE_TILE, LANE_TILE)
    kio = lax.broadcasted_iota(jnp.int32, (N_KEYS, LANE_TILE), 0).astype(jnp.float32)

    def head_body(h, carry):
        n_plane = jnp.zeros((N_KEYS, LANE_TILE), jnp.float32)
        rank2 = jnp.full((N_KEYS, LANE_TILE), float(PEER_TOPK), jnp.float32)
        for r in range(PEER_TOPK):
            rrow = pl.ds(r, 1)
            n_plane = jnp.where(kio == hi_ref[0, h, rrow, lanes], nr_ref[h, rrow, lanes], n_plane)
            rank2 = jnp.where(kio == hi_ref[1, h, rrow, lanes], float(r), rank2)
        row1 = pl.multiple_of(h * N_KEYS, N_KEYS)
        row2 = pl.multiple_of((PEER_HEADS + h) * N_KEYS, N_KEYS)
        top = pl.ds(0, 1)
        e1n = jnp.exp(st_ref[pl.ds(row1, N_KEYS), lanes] - hv_ref[0, h, top, lanes]) * invz_ref[h, top, lanes]
        e2 = jnp.exp(st_ref[pl.ds(row2, N_KEYS), lanes] - hv_ref[1, h, top, lanes])
        for g in range(N_KEYS // 8):
            n_ref[h, g, :, lanes] = n_plane[g * 8:(g + 1) * 8]
            e1n_ref[h, g, :, lanes] = e1n[g * 8:(g + 1) * 8]
        rank2_ref[h, :, lanes] = rank2
        e2_ref[h, :, lanes] = e2
        return carry
    lax.fori_loop(0, PEER_HEADS, head_body, 0)


def _peer_kernel(h2t_ref, kw_ref, u_ref, vt_ref, out_ref,
                 st_ref, hv_ref, hi_ref, vals_ref, nr_ref, invz_ref, n_ref, e1n_ref, rank2_ref, e2_ref, w_ref):
    j = pl.program_id(1)
    n_lane_tiles = PEER_TB // LANE_TILE

    @pl.when(j == 0)
    def _route():
        st_ref[...] = jnp.dot(kw_ref[...], h2t_ref[...], preferred_element_type=jnp.float32)
        for ts in range(n_lane_tiles):
            _peer_topk_rows(st_ref, hv_ref, hi_ref, vals_ref, ts)
            _peer_pair_counts(vals_ref, nr_ref, invz_ref, ts)
            _peer_planes(st_ref, hv_ref, hi_ref, nr_ref, invz_ref, n_ref, e1n_ref, rank2_ref, e2_ref, ts)
        out_ref[...] = jnp.zeros_like(out_ref)

    inv_sqrt2 = 1.0 / math.sqrt(2.0)
    for sc in range(PEER_EB // PEER_SUB):
        erow = pl.ds(sc * PEER_SUB, PEER_SUB)
        s = jnp.dot(u_ref[erow, :], h2t_ref[...], preferred_element_type=jnp.float32)
        for ab in range(PEER_SUB // N_KEYS):
            a_loc = sc * (PEER_SUB // N_KEYS) + ab
            for ts in range(n_lane_tiles):
                lanes = pl.ds(ts * LANE_TILE, LANE_TILE)
                gate = jnp.zeros((N_KEYS, LANE_TILE), jnp.float32)
                for h in range(PEER_HEADS):
                    n_row = n_ref[h, j, pl.ds(a_loc, 1), lanes]
                    e1_row = e1n_ref[h, j, pl.ds(a_loc, 1), lanes]
                    gate = gate + jnp.where(rank2_ref[h, :, lanes] < n_row, e2_ref[h, :, lanes], 0.0) * e1_row
                x = s[ab * N_KEYS:(ab + 1) * N_KEYS, ts * LANE_TILE:(ts + 1) * LANE_TILE]
                act = x * (lax.erf(x * inv_sqrt2) + 1.0) * 0.5
                w_ref[pl.ds(sc * PEER_SUB + ab * N_KEYS, N_KEYS), lanes] = (act * gate).astype(jnp.bfloat16)
        out_ref[...] += jnp.dot(vt_ref[:, erow], w_ref[erow, :], preferred_element_type=jnp.float32)


def peer_dense(h2t, kw, u_bf, vt_bf):
    d, t = h2t.shape
    assert t % PEER_TB == 0 and N_EXPERTS % PEER_EB == 0 and PEER_EB // N_KEYS == 8
    f32 = jnp.float32
    return pl.pallas_call(
        _peer_kernel,
        grid=(t // PEER_TB, N_EXPERTS // PEER_EB),
        in_specs=[
            pl.BlockSpec((d, PEER_TB), lambda i, j: (0, i)),
            pl.BlockSpec(kw.shape, lambda i, j: (0, 0)),
            pl.BlockSpec((PEER_EB, d), lambda i, j: (j, 0)),
            pl.BlockSpec((d, PEER_EB), lambda i, j: (0, j)),
        ],
        out_specs=pl.BlockSpec((d, PEER_TB), lambda i, j: (0, i)),
        out_shape=jax.ShapeDtypeStruct((d, t), f32),
        scratch_shapes=[
            pltpu.VMEM((2 * PEER_HEADS * N_KEYS, PEER_TB), f32),
            pltpu.VMEM((2, PEER_HEADS, PEER_TOPK, PEER_TB), f32),
            pltpu.VMEM((2, PEER_HEADS, PEER_TOPK, PEER_TB), f32),
            pltpu.VMEM((2, PEER_TOPK, PEER_HEADS, PEER_TB), f32),
            pltpu.VMEM((PEER_HEADS, PEER_TOPK, PEER_TB), f32),
            pltpu.VMEM((PEER_HEADS, 8, PEER_TB), f32),
            pltpu.VMEM((PEER_HEADS, N_KEYS // 8, 8, PEER_TB), f32),
            pltpu.VMEM((PEER_HEADS, N_KEYS // 8, 8, PEER_TB), f32),
            pltpu.VMEM((PEER_HEADS, N_KEYS, PEER_TB), f32),
            pltpu.VMEM((PEER_HEADS, N_KEYS, PEER_TB), f32),
            pltpu.VMEM((PEER_EB, PEER_TB), jnp.bfloat16),
        ],
        compiler_params=pltpu.CompilerParams(
            dimension_semantics=("arbitrary", "arbitrary"), vmem_limit_bytes=PEER_VMEM_LIMIT),
        name="peer_dense",
    )(h2t, kw, u_bf, vt_bf)


def peer_weights(w_q, keys, u, v):
    wq4 = w_q.reshape(D_MODEL, PEER_HEADS, 2, PEER_DQ // 2)
    kw = jnp.einsum('ikd,nhid->ihkn', keys, wq4, precision=lax.Precision.HIGHEST)
    kw = kw.reshape(2 * PEER_HEADS * N_KEYS, D_MODEL).astype(jnp.bfloat16)
    return kw, u.astype(jnp.bfloat16), v.T.astype(jnp.bfloat16)


def trunk_layer(x, cvec, ctx, w_mod, b_mod, g_norm1, g_norm2, w_in, w_gate, b_gate, g_q_norm, w_uq,
                g_kv_norm, w_ukv, w_attn_out, ssd_conv_w, ssd_conv_b, ssd_dt_bias, ssd_a_log, ssd_d,
                g_ssd_norm, w_ssd_out, cv_dw_w, cv_dw_b, cv_ln_g, cv_ln_b, w_cv_pw2, w_out, w_peer_q,
                peer_keys, peer_u, peer_v):
    b, L, _ = x.shape
    sh1, sc1, gt1, sh2, sc2, gt2 = modulation(cvec, w_mod, b_mod)
    h = rmsnorm(x, g_norm1) * (1 + sc1) + sh1
    zin = h @ w_in
    q_nope, q_rope = mla_queries(zin[..., OFF_CQ:OFF_CKV], g_q_norm, w_uq)
    ckv = rmsnorm(zin[..., OFF_CKV:OFF_KR], g_kv_norm)
    k_rope = zin[..., OFF_KR:OFF_Z]
    k_nope, v = mla_kv(ckv, w_ukv)
    if ctx is None:
        h0 = jnp.zeros((b, SSD_HEADS, SSD_HEAD_DIM, SSD_STATE), jnp.float32)
        h0_f, h0_b = h0, h0
        kn_all, kr_all, v_all = k_nope, k_rope, v
    else:
        ckv_c, kr_c, h0_f, h0_b = ctx
        cos, sin = axial_rope_tables(L)
        q_rope = apply_axial_rope(q_rope, cos, sin)
        kn_c, v_c = mla_kv(ckv_c.astype(x.dtype), w_ukv)
        kn_all = jnp.concatenate([kn_c, k_nope], axis=1)
        kr_all = jnp.concatenate([kr_c.astype(x.dtype), apply_axial_rope(k_rope, cos, sin)], axis=1)
        v_all = jnp.concatenate([v_c, v], axis=1)
    o_attn = mla_attention(q_nope, q_rope, kn_all, kr_all, v_all) @ w_attn_out
    y_ssd, h_f, h_b = ssd_bidirectional(zin[..., OFF_XBC:OFF_DT], zin[..., OFF_DT:OFF_GLU], ssd_conv_w, ssd_conv_b,
                                        ssd_dt_bias, ssd_a_log, ssd_d, h0_f, h0_b)
    yz = y_ssd * jax.nn.silu(zin[..., OFF_Z:OFF_XBC].astype(jnp.float32))
    o_ssd = rmsnorm(yz, g_ssd_norm).astype(x.dtype) @ w_ssd_out
    o_conv = conformer_conv(zin[..., OFF_GLU:N_IN], cv_dw_w, cv_dw_b, cv_ln_g, cv_ln_b, w_cv_pw2)
    gates = jax.nn.sigmoid((h @ w_gate + b_gate).astype(jnp.float32)).astype(x.dtype)
    g_a, g_s, g_c = jnp.split(gates, N_BRANCH, axis=-1)
    x = x + gt1 * ((g_a * o_attn + g_s * o_ssd + g_c * o_conv) @ w_out)
    h2 = rmsnorm(x, g_norm2) * (1 + sc2) + sh2
    h2t = h2.reshape(b * L, D_MODEL).T.astype(jnp.bfloat16)
    if ctx is None:
        return x, h2t, gt2, (ckv, k_rope, h_f.astype(x.dtype), h_b.astype(x.dtype))
    return x, h2t, gt2, None


def _rms_kernel(x_ref, g_ref, o_ref):
    x = x_ref[...]
    o_ref[...] = x * lax.rsqrt(jnp.mean(x * x, axis=-1, keepdims=True) + EPS) * g_ref[...]


def final_rmsnorm(x, g):
    shp = x.shape
    x2 = x.reshape(-1, shp[-1])
    t = x2.shape[0]
    tb = 512
    out = pl.pallas_call(
        _rms_kernel,
        grid=(t // tb,),
        in_specs=[pl.BlockSpec((tb, shp[-1]), lambda i: (i, 0)), pl.BlockSpec((1, shp[-1]), lambda i: (0, 0))],
        out_specs=pl.BlockSpec((tb, shp[-1]), lambda i: (i, 0)),
        out_shape=jax.ShapeDtypeStruct(x2.shape, x2.dtype),
        name="final_rmsnorm",
    )(x2, g.reshape(1, -1))
    return out.reshape(shp)


def kernel(x_prompt, x_sample, c, cache_ckv, cache_krope, state_ssd_fwd, state_ssd_bwd, c_ctx,
           w_mod, b_mod, g_norm1, g_norm2, w_in, w_gate, b_gate, g_q_norm, w_uq, g_kv_norm, w_ukv,
           w_attn_out, ssd_conv_w, ssd_conv_b, ssd_dt_bias, ssd_a_log, ssd_d, g_ssd_norm, w_ssd_out,
           cv_dw_w, cv_dw_b, cv_ln_g, cv_ln_b, w_cv_pw2, w_out, w_peer_q, peer_keys, peer_u, peer_v,
           g_final):
    y_p = x_prompt
    y_s = x_sample
    ckvs, krs, hfs, hbs = [], [], [], []
    for l in range(DEPTH):
        p = dict(w_mod=w_mod[l], b_mod=b_mod[l], g_norm1=g_norm1[l], g_norm2=g_norm2[l], w_in=w_in[l],
                 w_gate=w_gate[l], b_gate=b_gate[l], g_q_norm=g_q_norm[l], w_uq=w_uq[l],
                 g_kv_norm=g_kv_norm[l], w_ukv=w_ukv[l], w_attn_out=w_attn_out[l],
                 ssd_conv_w=ssd_conv_w[l], ssd_conv_b=ssd_conv_b[l], ssd_dt_bias=ssd_dt_bias[l],
                 ssd_a_log=ssd_a_log[l], ssd_d=ssd_d[l], g_ssd_norm=g_ssd_norm[l], w_ssd_out=w_ssd_out[l],
                 cv_dw_w=cv_dw_w[l], cv_dw_b=cv_dw_b[l], cv_ln_g=cv_ln_g[l], cv_ln_b=cv_ln_b[l],
                 w_cv_pw2=w_cv_pw2[l], w_out=w_out[l], w_peer_q=w_peer_q[l], peer_keys=peer_keys[l],
                 peer_u=peer_u[l], peer_v=peer_v[l])
        y_p, h2t_p, gt2_p, ctx_out = trunk_layer(y_p, c_ctx[None, :], None, **p)
        ckv, kr, hf, hb = ctx_out
        ckvs.append(ckv)
        krs.append(kr)
        hfs.append(hf)
        hbs.append(hb)
        y_s, h2t_s, gt2_s, _ = trunk_layer(
            y_s, c, (cache_ckv[:, l], cache_krope[:, l], state_ssd_fwd[:, l], state_ssd_bwd[:, l]), **p)
        kw, u_bf, vt_bf = peer_weights(w_peer_q[l], peer_keys[l], peer_u[l], peer_v[l])
        peer_t = peer_dense(jnp.concatenate([h2t_p, h2t_s], axis=1), kw, u_bf, vt_bf)
        n_p = BATCH * SEQ
        y_p = y_p + gt2_p * peer_t[:, :n_p].T.reshape(y_p.shape)
        y_s = y_s + gt2_s * peer_t[:, n_p:].T.reshape(y_s.shape)
    y_prompt = final_rmsnorm(y_p, g_final)
    y_sample = final_rmsnorm(y_s, g_final)
    new_ckv = jnp.stack(ckvs, axis=1)
    new_krope = jnp.stack(krs, axis=1)
    new_ssd_fwd = jnp.stack(hfs, axis=1)
    new_ssd_bwd = jnp.stack(hbs, axis=1)
    return (y_prompt, y_sample, new_ckv, new_krope, new_ssd_fwd, new_ssd_bwd)
```
